```python
import jax, jax.numpy as jnp
from jax import lax
import numpy as np

D_MODEL = 1024
BATCH = 4
SEQ = 8192
DEPTH = 1

D_RNN = D_MODEL
RNN_HEADS = 8
RNN_HEAD_DIM = D_RNN // RNN_HEADS
RNN_CONV_WIDTH = 4
RGLRU_C = 8.0
D_CONV = D_MODEL
CONF_KERNEL = 31
D_FF = ((8 * D_MODEL // 3 + 127) // 128) * 128
N_BRANCH = 2
N_SUBLAYER = 3
FFN_RES_WEIGHT = 0.5
DEEPNORM_ALPHA = (2.0 * DEPTH) ** 0.25
DEEPNORM_BETA = (8.0 * DEPTH) ** -0.25
LN_EPS = 1e-5
IN_SPLITS = (D_RNN, 2 * D_RNN, 2 * D_RNN + 2 * D_CONV, 2 * D_RNN + 2 * D_CONV + D_MODEL)
D_IN = 2 * D_RNN + 2 * D_CONV + N_BRANCH * D_MODEL

kernel_name = "hybrid_rglru_conformer_deepnorm_adaln"


def layer_norm(x, g, b):
    xf = x.astype(jnp.float32)
    mu = jnp.mean(xf, axis=-1, keepdims=True)
    xc = xf - mu
    var = jnp.mean(jnp.square(xc), axis=-1, keepdims=True)
    y = (xc * lax.rsqrt(var + LN_EPS)).astype(x.dtype)
    return y * g + b


def modulate(x, shift, scale):
    return x * (1 + scale[:, None, :]) + shift[:, None, :]


def causal_depthwise_conv(x, w, b):
    k = w.shape[0]
    y = lax.conv_general_dilated(
        x, w[:, None, :], window_strides=(1,), padding=[(k - 1, 0)],
        dimension_numbers=('NWC', 'WIO', 'NWC'), feature_group_count=x.shape[-1])
    return y + b


def swiglu(u, w_gate, w_up, w_down):
    return (jax.nn.silu(u @ w_gate) * (u @ w_up)) @ w_down


def rg_lru(x, w_r, b_r, w_i, b_i, lam):
    bsz, seq, _ = x.shape
    xh = x.reshape(bsz, seq, RNN_HEADS, RNN_HEAD_DIM)
    r = jax.nn.sigmoid(jnp.einsum('bshd,hde->bshe', xh, w_r).reshape(bsz, seq, D_RNN) + b_r)
    i = jax.nn.sigmoid(jnp.einsum('bshd,hde->bshe', xh, w_i).reshape(bsz, seq, D_RNN) + b_i)
    log_a = (-RGLRU_C * jax.nn.softplus(-lam.astype(jnp.float32))) * r.astype(jnp.float32)
    a = jnp.exp(log_a)
    bterm = jnp.sqrt(-jnp.expm1(2.0 * log_a)) * (i * x).astype(jnp.float32)

    def combine(left, right):
        a1, b1 = left
        a2, b2 = right
        return a1 * a2, a2 * b1 + b2

    _, h = lax.associative_scan(combine, (a, bterm), axis=1)
    return h.astype(x.dtype)


def setup_inputs(seed: int = 0) -> dict:
    key = jax.random.key(seed)
    ks = jax.random.split(key, 40)
    f32 = jnp.float32

    def nrm(k, shape, scale):
        return jax.random.normal(k, shape, f32) * scale

    def gain(k, n):
        return 1.0 + 0.02 * jax.random.normal(k, (n,), f32)

    a0 = jax.random.uniform(ks[14], (D_RNN,), f32, 0.9, 0.999) ** (1.0 / RGLRU_C)
    lam = jnp.log(a0) - jnp.log1p(-a0)
    return {
        'x': nrm(ks[0], (BATCH, SEQ, D_MODEL), 1.0),
        'c': nrm(ks[1], (BATCH, D_MODEL), 1.0),
        'ada_w': nrm(ks[2], (D_MODEL, 3 * N_SUBLAYER * D_MODEL), 0.2 * D_MODEL ** -0.5),
        'ada_b': nrm(ks[3], (3 * N_SUBLAYER * D_MODEL,), 0.01),
        'ffn1_w_gate': nrm(ks[4], (D_MODEL, D_FF), D_MODEL ** -0.5),
        'ffn1_w_up': nrm(ks[5], (D_MODEL, D_FF), D_MODEL ** -0.5),
        'ffn1_w_down': nrm(ks[6], (D_FF, D_MODEL), DEEPNORM_BETA * D_FF ** -0.5),
        'ln1_g': gain(ks[7], D_MODEL),
        'ln1_b': nrm(ks[8], (D_MODEL,), 0.02),
        'w_in': nrm(ks[9], (D_MODEL, D_IN), D_MODEL ** -0.5),
        'rnn_conv_w': nrm(ks[10], (RNN_CONV_WIDTH, D_RNN), RNN_CONV_WIDTH ** -0.5),
        'rnn_conv_b': nrm(ks[11], (D_RNN,), 0.01),
        'rglru_w_r': nrm(ks[12], (RNN_HEADS, RNN_HEAD_DIM, RNN_HEAD_DIM), RNN_HEAD_DIM ** -0.5),
        'rglru_b_r': nrm(ks[13], (D_RNN,), 0.01),
        'rglru_w_i': nrm(ks[15], (RNN_HEADS, RNN_HEAD_DIM, RNN_HEAD_DIM), RNN_HEAD_DIM ** -0.5),
        'rglru_b_i': nrm(ks[16], (D_RNN,), 0.01),
        'rglru_lambda': lam,
        'rnn_w_proj': nrm(ks[17], (D_RNN, D_MODEL), D_RNN ** -0.5),
        'conf_dw_w': nrm(ks[18], (CONF_KERNEL, D_CONV), CONF_KERNEL ** -0.5),
        'conf_dw_b': nrm(ks[19], (D_CONV,), 0.01),
        'conf_ln_g': gain(ks[20], D_CONV),
        'conf_ln_b': nrm(ks[21], (D_CONV,), 0.02),
        'conf_w_proj': nrm(ks[22], (D_CONV, D_MODEL), D_CONV ** -0.5),
        'mix_w_out': nrm(ks[23], (D_MODEL, D_MODEL), DEEPNORM_BETA * D_MODEL ** -0.5),
        'ln2_g': gain(ks[24], D_MODEL),
        'ln2_b': nrm(ks[25], (D_MODEL,), 0.02),
        'ffn2_w_gate': nrm(ks[26], (D_MODEL, D_FF), D_MODEL ** -0.5),
        'ffn2_w_up': nrm(ks[27], (D_MODEL, D_FF), D_MODEL ** -0.5),
        'ffn2_w_down': nrm(ks[28], (D_FF, D_MODEL), DEEPNORM_BETA * D_FF ** -0.5),
        'ln3_g': gain(ks[29], D_MODEL),
        'ln3_b': nrm(ks[30], (D_MODEL,), 0.02),
    }


def reference(x, c, ada_w, ada_b, ffn1_w_gate, ffn1_w_up, ffn1_w_down, ln1_g, ln1_b,
              w_in, rnn_conv_w, rnn_conv_b, rglru_w_r, rglru_b_r, rglru_w_i, rglru_b_i,
              rglru_lambda, rnn_w_proj, conf_dw_w, conf_dw_b, conf_ln_g, conf_ln_b,
              conf_w_proj, mix_w_out, ln2_g, ln2_b, ffn2_w_gate, ffn2_w_up, ffn2_w_down,
              ln3_g, ln3_b):
    mod = jax.nn.silu(c) @ ada_w + ada_b
    sh1, sc1, g1, sh2, sc2, g2, sh3, sc3, g3 = jnp.split(mod, 3 * N_SUBLAYER, axis=-1)

    for _ in range(DEPTH):
        u = modulate(x, sh1, sc1)
        y = swiglu(u, ffn1_w_gate, ffn1_w_up, ffn1_w_down)
        x = layer_norm(DEEPNORM_ALPHA * x + FFN_RES_WEIGHT * (1 + g1[:, None, :]) * y, ln1_g, ln1_b)

        u = modulate(x, sh2, sc2)
        proj = u @ w_in
        xr, gr, conf_in, gate_a, gate_b = jnp.split(proj, IN_SPLITS, axis=-1)

        xr = causal_depthwise_conv(xr, rnn_conv_w, rnn_conv_b)
        hr = rg_lru(xr, rglru_w_r, rglru_b_r, rglru_w_i, rglru_b_i, rglru_lambda)
        ya = (hr * jax.nn.gelu(gr)) @ rnn_w_proj

        glu = conf_in[..., :D_CONV] * jax.nn.sigmoid(conf_in[..., D_CONV:])
        hb = causal_depthwise_conv(glu, conf_dw_w, conf_dw_b)
        hb = jax.nn.silu(layer_norm(hb, conf_ln_g, conf_ln_b))
        yb = hb @ conf_w_proj

        merged = jax.nn.sigmoid(gate_a) * ya + jax.nn.sigmoid(gate_b) * yb
        out = merged @ mix_w_out
        x = layer_norm(DEEPNORM_ALPHA * x + (1 + g2[:, None, :]) * out, ln2_g, ln2_b)

        u = modulate(x, sh3, sc3)
        y = swiglu(u, ffn2_w_gate, ffn2_w_up, ffn2_w_down)
        x = layer_norm(DEEPNORM_ALPHA * x + FFN_RES_WEIGHT * (1 + g3[:, None, :]) * y, ln3_g, ln3_b)
    return x
```

```python
import functools

import jax
import jax.numpy as jnp
from jax import lax
from jax.experimental import pallas as pl
from jax.experimental.pallas import tpu as pltpu

RNN_HEADS = 8
RNN_CONV_WIDTH = 4
CONF_KERNEL = 31
RGLRU_C = 8.0
N_SUBLAYER = 3
FFN_RES_WEIGHT = 0.5
DEPTH = 1
DEEPNORM_ALPHA = (2.0 * DEPTH) ** 0.25
LN_EPS = 1e-5

SUBLANES = 8
LANES = 128
MXU_DIM = 256
VMEM_LIMIT_BYTES = 56 * 1024 * 1024

FFN_ROWS = 512
MIX_ROWS = 256
CHUNK = 8
BF16_CHUNK = 16
XR_HALO = 8
GLU_HALO = 32

BF16 = jnp.bfloat16
F32 = jnp.float32


def _const_spec(shape):
    nd = len(shape)
    return pl.BlockSpec(shape, lambda *_: (0,) * nd, pipeline_mode=pl.Buffered(1))


def _layer_norm(z, g, b):
    mu = jnp.mean(z, axis=-1, keepdims=True)
    zc = z - mu
    var = jnp.mean(zc * zc, axis=-1, keepdims=True)
    return zc * lax.rsqrt(var + LN_EPS) * g + b


def _sigmoid(x):
    return 1.0 / (1.0 + jnp.exp(-x))


def _ada_kernel(c_ref, w_ref, b_ref, o_ref):
    c = c_ref[...]
    a = (c * _sigmoid(c)).astype(BF16)
    o_ref[...] = jnp.dot(a, w_ref[...].astype(BF16), preferred_element_type=F32) + b_ref[...]


def _ada_call(c, ada_w, ada_b):
    bsz, d = c.shape
    n = ada_w.shape[1]
    bn = d
    return pl.pallas_call(
        _ada_kernel,
        grid=(n // bn,),
        in_specs=[
            pl.BlockSpec((bsz, d), lambda j: (0, 0)),
            pl.BlockSpec((d, bn), lambda j: (0, j)),
            pl.BlockSpec((1, bn), lambda j: (0, j)),
        ],
        out_specs=pl.BlockSpec((bsz, bn), lambda j: (0, j)),
        out_shape=jax.ShapeDtypeStruct((bsz, n), F32),
        compiler_params=pltpu.CompilerParams(dimension_semantics=("arbitrary",)),
        name="ada_mod",
    )(c, ada_w, ada_b.reshape(1, n))


def _ffn_kernel(sub, x_ref, mod_ref, wg_ref, wu_ref, wd_ref, ln_ref, o_ref):
    x = x_ref[0]
    sh = mod_ref[0, 3 * sub + 0:3 * sub + 1, :]
    sc = mod_ref[0, 3 * sub + 1:3 * sub + 2, :]
    gt = mod_ref[0, 3 * sub + 2:3 * sub + 3, :]
    u = (x * (1.0 + sc) + sh).astype(BF16)
    g = jnp.dot(u, wg_ref[...], preferred_element_type=F32)
    p = jnp.dot(u, wu_ref[...], preferred_element_type=F32)
    h = (g * _sigmoid(g) * p).astype(BF16)
    y = jnp.dot(h, wd_ref[...], preferred_element_type=F32)
    z = DEEPNORM_ALPHA * x + (FFN_RES_WEIGHT * (1.0 + gt)) * y
    o_ref[0] = _layer_norm(z, ln_ref[0:1, :], ln_ref[1:2, :])


def _ffn_call(sub, x, mod, wg, wu, wd, ln, name):
    bsz, seq, d = x.shape
    dff = wg.shape[1]
    tm = min(FFN_ROWS, seq)
    return pl.pallas_call(
        functools.partial(_ffn_kernel, sub),
        grid=(bsz, seq // tm),
        in_specs=[
            pl.BlockSpec((1, tm, d), lambda b, s: (b, s, 0)),
            pl.BlockSpec((1, 3 * N_SUBLAYER, d), lambda b, s: (b, 0, 0)),
            _const_spec((d, dff)),
            _const_spec((d, dff)),
            _const_spec((dff, d)),
            _const_spec((2, d)),
        ],
        out_specs=pl.BlockSpec((1, tm, d), lambda b, s: (b, s, 0)),
        out_shape=jax.ShapeDtypeStruct((bsz, seq, d), F32),
        compiler_params=pltpu.CompilerParams(
            dimension_semantics=("arbitrary", "arbitrary"),
            vmem_limit_bytes=VMEM_LIMIT_BYTES),
        name=name,
    )(x, mod, wg, wu, wd, ln)


V_RCONV_B, V_BR, V_BI, V_LAM, V_CDW_B, V_CLN_G, V_CLN_B, V_LN2_G, V_LN2_B = range(9)
N_VEC = 9


def _mixer_kernel(x_ref, mod_ref, win_ref, rcw_ref, wri_ref, wrp_ref, cdw_ref, wcp_ref,
                  wmo_ref, vec_ref, o_ref,
                  u_s, xr_s, gr_s, ca_s, cb_s, ga_s, gb_s, xc_s, rp_s, ip_s,
                  a_s, b_s, h_s, glu_s, ya_in_s, yb_in_s, h_carry):
    tm = x_ref.shape[1]
    d = x_ref.shape[2]
    n_chunks = tm // CHUNK
    n_slabs = d // LANES
    seq_step = pl.program_id(1)

    def lanes(j):
        return slice(j * LANES, (j + 1) * LANES)

    @pl.when(seq_step == 0)
    def _():
        xr_s[:, 0:XR_HALO, :] = jnp.zeros((n_slabs, XR_HALO, LANES), F32)
        glu_s[:, 0:GLU_HALO, :] = jnp.zeros((n_slabs, GLU_HALO, LANES), F32)
        h_carry[...] = jnp.zeros_like(h_carry)

    sh = mod_ref[0, 3:4, :]
    sc = mod_ref[0, 4:5, :]
    gt = mod_ref[0, 5:6, :]

    def vec(i):
        return vec_ref[i:i + 1, :]

    u_s[...] = (x_ref[0] * (1.0 + sc) + sh).astype(BF16)

    def proj(g):
        return jnp.dot(u_s[...], win_ref[:, g * d:(g + 1) * d], preferred_element_type=F32)

    xr = proj(0)
    for j in range(n_slabs):
        xr_s[j, XR_HALO:XR_HALO + tm, :] = xr[:, lanes(j)]
    gr_s[...] = proj(1)
    ca_s[...] = proj(2)
    cb_s[...] = proj(3)
    ga_s[...] = proj(4)
    gb_s[...] = proj(5)

    def dwconv(w_ref, bias, taps, halo, win_s, out_s):
        for j in range(n_slabs):
            w = [jnp.broadcast_to(w_ref[k:k + 1, lanes(j)], (CHUNK, LANES)) for k in range(taps)]
            b = jnp.broadcast_to(bias[:, lanes(j)], (CHUNK, LANES))

            def body(i, carry, j=j, w=w, b=b):
                r0 = pl.multiple_of(i * CHUNK, CHUNK)
                acc = b
                for k in range(taps):
                    acc = acc + w[k] * win_s[j, pl.ds(r0 + (halo - (taps - 1) + k), CHUNK), :]
                out_s[pl.ds(r0, CHUNK), lanes(j)] = acc
                return carry

            lax.fori_loop(0, n_chunks, body, 0)

    dwconv(rcw_ref, vec(V_RCONV_B), RNN_CONV_WIDTH, XR_HALO, xr_s, xc_s)

    n_pairs = d // MXU_DIM
    for p in range(n_pairs):
        cols = slice(p * MXU_DIM, (p + 1) * MXU_DIM)
        ri = jnp.dot(xc_s[:, cols].astype(BF16), wri_ref[p], preferred_element_type=F32)
        rp_s[:, cols] = ri[:, :MXU_DIM]
        ip_s[:, cols] = ri[:, MXU_DIM:]

    nlam = -vec(V_LAM)
    softplus = jnp.maximum(nlam, 0.0) + jnp.log1p(jnp.exp(-jnp.abs(nlam)))
    c_lam = -RGLRU_C * softplus

    def gate_body(i, carry):
        rows = pl.ds(pl.multiple_of(i * CHUNK, CHUNK), CHUNK)
        r = _sigmoid(rp_s[rows, :] + vec(V_BR))
        ig = _sigmoid(ip_s[rows, :] + vec(V_BI))
        log_a = c_lam * r
        a = jnp.exp(log_a)
        one_m_a2 = -jnp.tanh(log_a) * (1.0 + a * a)
        a_s[rows, :] = a
        b_s[rows, :] = jnp.sqrt(one_m_a2) * (ig * xc_s[rows, :])
        return carry

    lax.fori_loop(0, n_chunks, gate_body, 0)

    def scan_body(t, h):
        h = a_s[pl.ds(t, 1), :] * h + b_s[pl.ds(t, 1), :]
        h_s[pl.ds(t, 1), :] = h
        return h

    h_last = lax.fori_loop(0, tm, scan_body, h_carry[0:1, :], unroll=8)
    h_carry[0:1, :] = h_last

    def act_body(i, carry):
        r0 = pl.multiple_of(i * BF16_CHUNK, BF16_CHUNK)
        rows = pl.ds(r0, BF16_CHUNK)
        ya_in_s[rows, :] = (h_s[rows, :] * jax.nn.gelu(gr_s[rows, :])).astype(BF16)
        glu = ca_s[rows, :] * _sigmoid(cb_s[rows, :])
        for j in range(n_slabs):
            glu_s[j, pl.ds(r0 + GLU_HALO, BF16_CHUNK), :] = glu[:, lanes(j)]
        return carry

    lax.fori_loop(0, tm // BF16_CHUNK, act_body, 0)

    hb_s = ca_s
    dwconv(cdw_ref, vec(V_CDW_B), CONF_KERNEL, GLU_HALO, glu_s, hb_s)

    def conf_ln_body(i, carry):
        rows = pl.ds(pl.multiple_of(i * BF16_CHUNK, BF16_CHUNK), BF16_CHUNK)
        hb = _layer_norm(hb_s[rows, :], vec(V_CLN_G), vec(V_CLN_B))
        yb_in_s[rows, :] = (hb * _sigmoid(hb)).astype(BF16)
        return carry

    lax.fori_loop(0, tm // BF16_CHUNK, conf_ln_body, 0)

    ya = jnp.dot(ya_in_s[...], wrp_ref[...], preferred_element_type=F32)
    yb = jnp.dot(yb_in_s[...], wcp_ref[...], preferred_element_type=F32)
    merged = (_sigmoid(ga_s[...]) * ya + _sigmoid(gb_s[...]) * yb).astype(BF16)
    out = jnp.dot(merged, wmo_ref[...], preferred_element_type=F32)
    z = DEEPNORM_ALPHA * x_ref[0] + (1.0 + gt) * out
    o_ref[0] = _layer_norm(z, vec(V_LN2_G), vec(V_LN2_B))

    xr_s[:, 0:XR_HALO, :] = xr_s[:, tm:tm + XR_HALO, :]
    glu_s[:, 0:GLU_HALO, :] = glu_s[:, tm:tm + GLU_HALO, :]


def _mixer_call(x, mod, w_in, rcw, wri, wrp, cdw, wcp, wmo, vecs):
    bsz, seq, d = x.shape
    tm = min(MIX_ROWS, seq)
    f32_tile = pltpu.VMEM((tm, d), F32)
    return pl.pallas_call(
        _mixer_kernel,
        grid=(bsz, seq // tm),
        in_specs=[
            pl.BlockSpec((1, tm, d), lambda b, s: (b, s, 0)),
            pl.BlockSpec((1, 3 * N_SUBLAYER, d), lambda b, s: (b, 0, 0)),
            _const_spec(w_in.shape),
            _const_spec(rcw.shape),
            _const_spec(wri.shape),
            _const_spec(wrp.shape),
            _const_spec(cdw.shape),
            _const_spec(wcp.shape),
            _const_spec(wmo.shape),
            _const_spec(vecs.shape),
        ],
        out_specs=pl.BlockSpec((1, tm, d), lambda b, s: (b, s, 0)),
        out_shape=jax.ShapeDtypeStruct((bsz, seq, d), F32),
        scratch_shapes=[
            pltpu.VMEM((tm, d), BF16),
            pltpu.VMEM((d // LANES, tm + XR_HALO, LANES), F32),
            f32_tile, f32_tile, f32_tile, f32_tile, f32_tile,
            f32_tile, f32_tile, f32_tile,
            f32_tile, f32_tile, f32_tile,
            pltpu.VMEM((d // LANES, tm + GLU_HALO, LANES), F32),
            pltpu.VMEM((tm, d), BF16),
            pltpu.VMEM((tm, d), BF16),
            pltpu.VMEM((SUBLANES, d), F32),
        ],
        compiler_params=pltpu.CompilerParams(
            dimension_semantics=("arbitrary", "arbitrary"),
            vmem_limit_bytes=VMEM_LIMIT_BYTES),
        name="mixer",
    )(x, mod, w_in, rcw, wri, wrp, cdw, wcp, wmo, vecs)


def _pair_block_diag(w):
    heads, hd, _ = w.shape
    w = w.reshape(heads // 2, 2, hd, hd)
    z = jnp.zeros_like(w[:, 0])
    top = jnp.concatenate([w[:, 0], z], axis=-1)
    bot = jnp.concatenate([z, w[:, 1]], axis=-1)
    return jnp.concatenate([top, bot], axis=-2)


def kernel(x, c, ada_w, ada_b, ffn1_w_gate, ffn1_w_up, ffn1_w_down, ln1_g, ln1_b, w_in, rnn_conv_w, rnn_conv_b, rglru_w_r, rglru_b_r, rglru_w_i, rglru_b_i, rglru_lambda, rnn_w_proj, conf_dw_w, conf_dw_b, conf_ln_g, conf_ln_b, conf_w_proj, mix_w_out, ln2_g, ln2_b, ffn2_w_gate, ffn2_w_up, ffn2_w_down, ln3_g, ln3_b):
    bsz, seq, d = x.shape
    assert d % MXU_DIM == 0 and seq % min(FFN_ROWS, seq) == 0 and seq % min(MIX_ROWS, seq) == 0
    assert rglru_w_r.shape == (RNN_HEADS, d // RNN_HEADS, d // RNN_HEADS) and 2 * (d // RNN_HEADS) == MXU_DIM

    mod = _ada_call(c, ada_w, ada_b).reshape(bsz, 3 * N_SUBLAYER, d)

    bf = lambda w: w.astype(BF16)
    wri = jnp.concatenate([_pair_block_diag(rglru_w_r), _pair_block_diag(rglru_w_i)], axis=-1)
    vecs = jnp.stack([rnn_conv_b, rglru_b_r, rglru_b_i, rglru_lambda, conf_dw_b,
                      conf_ln_g, conf_ln_b, ln2_g, ln2_b]).astype(F32)

    for _ in range(DEPTH):
        x = _ffn_call(0, x, mod, bf(ffn1_w_gate), bf(ffn1_w_up), bf(ffn1_w_down),
                      jnp.stack([ln1_g, ln1_b]), "ffn1")
        x = _mixer_call(x, mod, bf(w_in), rnn_conv_w, bf(wri), bf(rnn_w_proj), conf_dw_w,
                        bf(conf_w_proj), bf(mix_w_out), vecs)
        x = _ffn_call(2, x, mod, bf(ffn2_w_gate), bf(ffn2_w_up), bf(ffn2_w_down),
                      jnp.stack([ln3_g, ln3_b]), "ffn2")
    return x
```

```python
import functools

import jax
import jax.numpy as jnp
from jax import lax
from jax.experimental import pallas as pl
from jax.experimental.pallas import tpu as pltpu

RNN_HEADS = 8
RNN_CONV_WIDTH = 4
CONF_KERNEL = 31
RGLRU_C = 8.0
N_SUBLAYER = 3
FFN_RES_WEIGHT = 0.5
DEPTH = 1
DEEPNORM_ALPHA = (2.0 * DEPTH) ** 0.25
LN_EPS = 1e-5

SUBLANES = 8
LANES = 128
MXU_DIM = 256
VMEM_LIMIT_BYTES = 56 * 1024 * 1024

FFN_ROWS = 512
MIX_ROWS = 256
CHUNK = 8
BF16_CHUNK = 16
CONV_ROWS = 64
LN_ROWS = 64
XR_HALO = 8
GLU_HALO = 32

BF16 = jnp.bfloat16
F32 = jnp.float32


def _const_spec(shape):
    nd = len(shape)
    return pl.BlockSpec(shape, lambda *_: (0,) * nd, pipeline_mode=pl.Buffered(1))


def _layer_norm(z, g, b):
    mu = jnp.mean(z, axis=-1, keepdims=True)
    zc = z - mu
    var = jnp.mean(zc * zc, axis=-1, keepdims=True)
    return zc * lax.rsqrt(var + LN_EPS) * g + b


def _sigmoid(x):
    return 1.0 / (1.0 + jnp.exp(-x))


def _ada_kernel(c_ref, w_ref, b_ref, o_ref):
    c = c_ref[...]
    a = (c * _sigmoid(c)).astype(BF16)
    o_ref[...] = jnp.dot(a, w_ref[...].astype(BF16), preferred_element_type=F32) + b_ref[...]


def _ada_call(c, ada_w, ada_b):
    bsz, d = c.shape
    n = ada_w.shape[1]
    bn = d
    return pl.pallas_call(
        _ada_kernel,
        grid=(n // bn,),
        in_specs=[
            pl.BlockSpec((bsz, d), lambda j: (0, 0)),
            pl.BlockSpec((d, bn), lambda j: (0, j)),
            pl.BlockSpec((1, bn), lambda j: (0, j)),
        ],
        out_specs=pl.BlockSpec((bsz, bn), lambda j: (0, j)),
        out_shape=jax.ShapeDtypeStruct((bsz, n), F32),
        compiler_params=pltpu.CompilerParams(dimension_semantics=("arbitrary",)),
        name="ada_mod",
    )(c, ada_w, ada_b.reshape(1, n))


def _ffn_kernel(sub, x_ref, mod_ref, wg_ref, wu_ref, wd_ref, ln_ref, o_ref):
    x = x_ref[0]
    sh = mod_ref[0, 3 * sub + 0:3 * sub + 1, :]
    sc = mod_ref[0, 3 * sub + 1:3 * sub + 2, :]
    gt = mod_ref[0, 3 * sub + 2:3 * sub + 3, :]
    u = (x * (1.0 + sc) + sh).astype(BF16)
    g = jnp.dot(u, wg_ref[...], preferred_element_type=F32)
    p = jnp.dot(u, wu_ref[...], preferred_element_type=F32)
    h = (g * _sigmoid(g) * p).astype(BF16)
    y = jnp.dot(h, wd_ref[...], preferred_element_type=F32)
    z = DEEPNORM_ALPHA * x + (FFN_RES_WEIGHT * (1.0 + gt)) * y
    o_ref[0] = _layer_norm(z, ln_ref[0:1, :], ln_ref[1:2, :])


def _ffn_call(sub, x, mod, wg, wu, wd, ln, name):
    bsz, seq, d = x.shape
    dff = wg.shape[1]
    tm = min(FFN_ROWS, seq)
    return pl.pallas_call(
        functools.partial(_ffn_kernel, sub),
        grid=(bsz, seq // tm),
        in_specs=[
            pl.BlockSpec((1, tm, d), lambda b, s: (b, s, 0)),
            pl.BlockSpec((1, 3 * N_SUBLAYER, d), lambda b, s: (b, 0, 0)),
            _const_spec((d, dff)),
            _const_spec((d, dff)),
            _const_spec((dff, d)),
            _const_spec((2, d)),
        ],
        out_specs=pl.BlockSpec((1, tm, d), lambda b, s: (b, s, 0)),
        out_shape=jax.ShapeDtypeStruct((bsz, seq, d), F32),
        compiler_params=pltpu.CompilerParams(
            dimension_semantics=("arbitrary", "arbitrary"),
            vmem_limit_bytes=VMEM_LIMIT_BYTES),
        name=name,
    )(x, mod, wg, wu, wd, ln)


V_RCONV_B, V_BR, V_BI, V_LAM, V_CDW_B, V_CLN_G, V_CLN_B, V_LN2_G, V_LN2_B = range(9)
N_VEC = 9
BC_BR, BC_BI, BC_CLAM, BC_CLN_G, BC_CLN_B = range(5)
N_BC = 5


def _mixer_kernel(x_ref, mod_ref, win_ref, rcw_ref, wri_ref, wrp_ref, cdw_ref, wcp_ref,
                  wmo_ref, vec_ref, o_ref,
                  u_s, xr_s, gr_s, ca_s, cb_s, ga_s, gb_s, xc_s, rp_s, ip_s,
                  a_s, b_s, h_s, glu_s, ya_in_s, yb_in_s, h_carry, bc_s):
    tm = x_ref.shape[1]
    d = x_ref.shape[2]
    n_chunks = tm // CHUNK
    n_slabs = d // LANES
    seq_step = pl.program_id(1)

    def lanes(j):
        return slice(j * LANES, (j + 1) * LANES)

    @pl.when(seq_step == 0)
    def _():
        xr_s[:, 0:XR_HALO, :] = jnp.zeros((n_slabs, XR_HALO, LANES), F32)
        glu_s[:, 0:GLU_HALO, :] = jnp.zeros((n_slabs, GLU_HALO, LANES), F32)
        h_carry[...] = jnp.zeros_like(h_carry)

    sh = mod_ref[0, 3:4, :]
    sc = mod_ref[0, 4:5, :]
    gt = mod_ref[0, 5:6, :]

    def vec(i):
        return vec_ref[i:i + 1, :]

    u_s[...] = (x_ref[0] * (1.0 + sc) + sh).astype(BF16)

    def proj(g):
        return jnp.dot(u_s[...], win_ref[:, g * d:(g + 1) * d], preferred_element_type=F32)

    xr = proj(0)
    for j in range(n_slabs):
        xr_s[j, XR_HALO:XR_HALO + tm, :] = xr[:, lanes(j)]
    gr_s[...] = proj(1)
    ca_s[...] = proj(2)
    cb_s[...] = proj(3)
    ga_s[...] = proj(4)
    gb_s[...] = proj(5)

    def dwconv(w_ref, bias, taps, halo, win_s, out_s):
        groups = CONV_ROWS // CHUNK
        for j in range(n_slabs):
            def body(i, carry, j=j):
                r0 = pl.multiple_of(i * CONV_ROWS, CONV_ROWS)
                b = jnp.broadcast_to(bias[:, lanes(j)], (CHUNK, LANES))
                accs = [b] * groups
                for k in range(taps):
                    w = jnp.broadcast_to(w_ref[k:k + 1, lanes(j)], (CHUNK, LANES))
                    off = halo - (taps - 1) + k
                    for g in range(groups):
                        accs[g] = accs[g] + w * win_s[j, pl.ds(r0 + (g * CHUNK + off), CHUNK), :]
                for g in range(groups):
                    out_s[pl.ds(r0 + g * CHUNK, CHUNK), lanes(j)] = accs[g]
                return carry

            lax.fori_loop(0, tm // CONV_ROWS, body, 0)

    dwconv(rcw_ref, vec(V_RCONV_B), RNN_CONV_WIDTH, XR_HALO, xr_s, xc_s)

    n_pairs = d // MXU_DIM
    for p in range(n_pairs):
        cols = slice(p * MXU_DIM, (p + 1) * MXU_DIM)
        ri = jnp.dot(xc_s[:, cols].astype(BF16), wri_ref[p], preferred_element_type=F32)
        rp_s[:, cols] = ri[:, :MXU_DIM]
        ip_s[:, cols] = ri[:, MXU_DIM:]

    @pl.when(seq_step == 0)
    def _():
        nlam = -vec(V_LAM)
        softplus = jnp.maximum(nlam, 0.0) + jnp.log1p(jnp.exp(-jnp.abs(nlam)))
        rows = {BC_BR: vec(V_BR), BC_BI: vec(V_BI), BC_CLAM: -RGLRU_C * softplus,
                BC_CLN_G: vec(V_CLN_G), BC_CLN_B: vec(V_CLN_B)}
        for i, v in rows.items():
            bc_s[i] = jnp.broadcast_to(v, (CHUNK, d))

    def gate_body(i, carry):
        rows = pl.ds(pl.multiple_of(i * CHUNK, CHUNK), CHUNK)
        r = _sigmoid(rp_s[rows, :] + bc_s[BC_BR])
        ig = _sigmoid(ip_s[rows, :] + bc_s[BC_BI])
        log_a = bc_s[BC_CLAM] * r
        a = jnp.exp(log_a)
        one_m_a2 = -jnp.tanh(log_a) * (1.0 + a * a)
        a_s[rows, :] = a
        b_s[rows, :] = jnp.sqrt(one_m_a2) * (ig * xc_s[rows, :])
        return carry

    lax.fori_loop(0, n_chunks, gate_body, 0)

    def scan_body(t, h):
        h = a_s[pl.ds(t, 1), :] * h + b_s[pl.ds(t, 1), :]
        h_s[pl.ds(t, 1), :] = h
        return h

    h_last = lax.fori_loop(0, tm, scan_body, h_carry[0:1, :], unroll=8)
    h_carry[0:1, :] = h_last

    def act_body(i, carry):
        r0 = pl.multiple_of(i * BF16_CHUNK, BF16_CHUNK)
        rows = pl.ds(r0, BF16_CHUNK)
        ya_in_s[rows, :] = (h_s[rows, :] * jax.nn.gelu(gr_s[rows, :])).astype(BF16)
        glu = ca_s[rows, :] * _sigmoid(cb_s[rows, :])
        for j in range(n_slabs):
            glu_s[j, pl.ds(r0 + GLU_HALO, BF16_CHUNK), :] = glu[:, lanes(j)]
        return carry

    lax.fori_loop(0, tm // BF16_CHUNK, act_body, 0)

    hb_s = ca_s
    dwconv(cdw_ref, vec(V_CDW_B), CONF_KERNEL, GLU_HALO, glu_s, hb_s)

    def conf_ln_body(i, carry):
        r0 = pl.multiple_of(i * LN_ROWS, LN_ROWS)
        g_row = jnp.concatenate([bc_s[BC_CLN_G]] * (BF16_CHUNK // CHUNK), axis=0)
        b_row = jnp.concatenate([bc_s[BC_CLN_B]] * (BF16_CHUNK // CHUNK), axis=0)
        for g in range(LN_ROWS // BF16_CHUNK):
            rows = pl.ds(r0 + g * BF16_CHUNK, BF16_CHUNK)
            hb = _layer_norm(hb_s[rows, :], g_row, b_row)
            yb_in_s[rows, :] = (hb * _sigmoid(hb)).astype(BF16)
        return carry

    lax.fori_loop(0, tm // LN_ROWS, conf_ln_body, 0)

    ya = jnp.dot(ya_in_s[...], wrp_ref[...], preferred_element_type=F32)
    yb = jnp.dot(yb_in_s[...], wcp_ref[...], preferred_element_type=F32)
    merged = (_sigmoid(ga_s[...]) * ya + _sigmoid(gb_s[...]) * yb).astype(BF16)
    out = jnp.dot(merged, wmo_ref[...], preferred_element_type=F32)
    z = DEEPNORM_ALPHA * x_ref[0] + (1.0 + gt) * out
    o_ref[0] = _layer_norm(z, vec(V_LN2_G), vec(V_LN2_B))

    xr_s[:, 0:XR_HALO, :] = xr_s[:, tm:tm + XR_HALO, :]
    glu_s[:, 0:GLU_HALO, :] = glu_s[:, tm:tm + GLU_HALO, :]


def _mixer_call(x, mod, w_in, rcw, wri, wrp, cdw, wcp, wmo, vecs):
    bsz, seq, d = x.shape
    tm = min(MIX_ROWS, seq)
    f32_tile = pltpu.VMEM((tm, d), F32)
    return pl.pallas_call(
        _mixer_kernel,
        grid=(bsz, seq // tm),
        in_specs=[
            pl.BlockSpec((1, tm, d), lambda b, s: (b, s, 0)),
            pl.BlockSpec((1, 3 * N_SUBLAYER, d), lambda b, s: (b, 0, 0)),
            _const_spec(w_in.shape),
            _const_spec(rcw.shape),
            _const_spec(wri.shape),
            _const_spec(wrp.shape),
            _const_spec(cdw.shape),
            _const_spec(wcp.shape),
            _const_spec(wmo.shape),
            _const_spec(vecs.shape),
        ],
        out_specs=pl.BlockSpec((1, tm, d), lambda b, s: (b, s, 0)),
        out_shape=jax.ShapeDtypeStruct((bsz, seq, d), F32),
        scratch_shapes=[
            pltpu.VMEM((tm, d), BF16),
            pltpu.VMEM((d // LANES, tm + XR_HALO, LANES), F32),
            f32_tile, f32_tile, f32_tile, f32_tile, f32_tile,
            f32_tile, f32_tile, f32_tile,
            f32_tile, f32_tile, f32_tile,
            pltpu.VMEM((d // LANES, tm + GLU_HALO, LANES), F32),
            pltpu.VMEM((tm, d), BF16),
            pltpu.VMEM((tm, d), BF16),
            pltpu.VMEM((SUBLANES, d), F32),
            pltpu.VMEM((N_BC, CHUNK, d), F32),
        ],
        compiler_params=pltpu.CompilerParams(
            dimension_semantics=("arbitrary", "arbitrary"),
            vmem_limit_bytes=VMEM_LIMIT_BYTES),
        name="mixer",
    )(x, mod, w_in, rcw, wri, wrp, cdw, wcp, wmo, vecs)


def _pair_block_diag(w):
    heads, hd, _ = w.shape
    w = w.reshape(heads // 2, 2, hd, hd)
    z = jnp.zeros_like(w[:, 0])
    top = jnp.concatenate([w[:, 0], z], axis=-1)
    bot = jnp.concatenate([z, w[:, 1]], axis=-1)
    return jnp.concatenate([top, bot], axis=-2)


def kernel(x, c, ada_w, ada_b, ffn1_w_gate, ffn1_w_up, ffn1_w_down, ln1_g, ln1_b, w_in, rnn_conv_w, rnn_conv_b, rglru_w_r, rglru_b_r, rglru_w_i, rglru_b_i, rglru_lambda, rnn_w_proj, conf_dw_w, conf_dw_b, conf_ln_g, conf_ln_b, conf_w_proj, mix_w_out, ln2_g, ln2_b, ffn2_w_gate, ffn2_w_up, ffn2_w_down, ln3_g, ln3_b):
    bsz, seq, d = x.shape
    assert d % MXU_DIM == 0 and seq % min(FFN_ROWS, seq) == 0 and seq % min(MIX_ROWS, seq) == 0
    assert rglru_w_r.shape == (RNN_HEADS, d // RNN_HEADS, d // RNN_HEADS) and 2 * (d // RNN_HEADS) == MXU_DIM

    mod = _ada_call(c, ada_w, ada_b).reshape(bsz, 3 * N_SUBLAYER, d)

    bf = lambda w: w.astype(BF16)
    wri = jnp.concatenate([_pair_block_diag(rglru_w_r), _pair_block_diag(rglru_w_i)], axis=-1)
    vecs = jnp.stack([rnn_conv_b, rglru_b_r, rglru_b_i, rglru_lambda, conf_dw_b,
                      conf_ln_g, conf_ln_b, ln2_g, ln2_b]).astype(F32)

    for _ in range(DEPTH):
        x = _ffn_call(0, x, mod, bf(ffn1_w_gate), bf(ffn1_w_up), bf(ffn1_w_down),
                      jnp.stack([ln1_g, ln1_b]), "ffn1")
        x = _mixer_call(x, mod, bf(w_in), rnn_conv_w, bf(wri), bf(rnn_w_proj), conf_dw_w,
                        bf(conf_w_proj), bf(mix_w_out), vecs)
        x = _ffn_call(2, x, mod, bf(ffn2_w_gate), bf(ffn2_w_up), bf(ffn2_w_down),
                      jnp.stack([ln3_g, ln3_b]), "ffn2")
    return x
```
